```python
import math
import jax, jax.numpy as jnp
from jax import lax
import numpy as np

D_MODEL = 1024
BATCH = 2
SEQ = 8192
DEPTH = 1

SSD_HEADS = 16
SSD_HEAD_DIM = 64
SSD_WIDTH = SSD_HEADS * SSD_HEAD_DIM
SSD_GROUPS = 4
SSD_STATE = 128
SSD_CONV = 4
SSD_CHUNK = 256
S5_GROUP_DIM = 16
S5_GROUPS = 32
S5_WIDTH = S5_GROUPS * S5_GROUP_DIM
S5_STATE = 64
MIX_WIDTH = SSD_WIDTH + S5_WIDTH
CONV_DIM = SSD_WIDTH + 2 * SSD_GROUPS * SSD_STATE
IN_PROJ = SSD_WIDTH + CONV_DIM + SSD_HEADS + S5_WIDTH
N_EXPERTS = 256
TOP_K = 8
N_EXPERT_GROUPS = 8
TOPK_GROUPS = 4
EXPERT_FF = 256
SHARED_FF = 256
ROUTED_SCALE = 2.5
MOE_BLOCK = 128
N_MOD = 6
EPS = 1e-6

kernel_name = "hymba_ssd_s5_moe_adaln_block"


def rms_norm(x, gain):
    xf = x.astype(jnp.float32)
    y = xf * lax.rsqrt(jnp.mean(xf * xf, axis=-1, keepdims=True) + EPS)
    return (y * gain.astype(jnp.float32)).astype(x.dtype)


def causal_depthwise_conv(x, w, b):
    ch = x.shape[-1]
    out = lax.conv_general_dilated(x, w[:, None, :].astype(x.dtype), window_strides=(1,),
                                   padding=[(SSD_CONV - 1, 0)],
                                   dimension_numbers=("NWC", "WIO", "NWC"),
                                   feature_group_count=ch)
    return out + b


def ssd_chunked(xdt, a, bm, cm):
    bsz, seq, n_heads, hd = xdt.shape
    n_grp, n_state = bm.shape[-2:]
    hpg = n_heads // n_grp
    pad = (-seq) % SSD_CHUNK
    if pad:
        xdt = jnp.pad(xdt, ((0, 0), (0, pad), (0, 0), (0, 0)))
        a = jnp.pad(a, ((0, 0), (0, pad), (0, 0)))
        bm = jnp.pad(bm, ((0, 0), (0, pad), (0, 0), (0, 0)))
        cm = jnp.pad(cm, ((0, 0), (0, pad), (0, 0), (0, 0)))
    n_chunks = (seq + pad) // SSD_CHUNK
    x = xdt.reshape(bsz, n_chunks, SSD_CHUNK, n_grp, hpg, hd)
    a = a.reshape(bsz, n_chunks, SSD_CHUNK, n_grp, hpg).transpose(0, 3, 4, 1, 2)
    bm = bm.reshape(bsz, n_chunks, SSD_CHUNK, n_grp, n_state)
    cm = cm.reshape(bsz, n_chunks, SSD_CHUNK, n_grp, n_state)
    a_cs = jnp.cumsum(a, axis=-1)
    causal = jnp.tril(jnp.ones((SSD_CHUNK, SSD_CHUNK), dtype=bool))
    seg = a_cs[..., :, None] - a_cs[..., None, :]
    decay_ls = jnp.exp(jnp.where(causal, seg, -jnp.inf))
    scores = jnp.einsum("bclgn,bcsgn->bgcls", cm, bm)
    y_diag = jnp.einsum("bghcls,bcsghp->bclghp", scores[:, :, None] * decay_ls, x)
    decay_states = jnp.exp(a_cs[..., -1:] - a_cs)
    states = jnp.einsum("bclgn,bghcl,bclghp->bcghpn", bm, decay_states, x)
    a_tot = a_cs[..., -1]

    def chunk_step(h, inp):
        st, at = inp
        return jnp.exp(at)[..., None, None] * h + st, h

    h0 = jnp.zeros(states.shape[:1] + states.shape[2:], states.dtype)
    _, prev_states = lax.scan(chunk_step, h0,
                              (states.transpose(1, 0, 2, 3, 4, 5), a_tot.transpose(3, 0, 1, 2)))
    y_off = jnp.einsum("bclgn,cbghpn,bghcl->bclghp", cm, prev_states, jnp.exp(a_cs))
    y = (y_diag + y_off).reshape(bsz, n_chunks * SSD_CHUNK, n_heads, hd)
    return y[:, :seq]


def s5_scan(u, a_re, a_im, log_dt, b_re, b_im, c_re, c_im, d_skip):
    f32 = jnp.float32
    u = u.astype(f32)
    dt = jnp.exp(log_dt.astype(f32))[:, None]
    lam_re = jnp.minimum(a_re.astype(f32), -1e-4)
    lam_im = a_im.astype(f32)
    mag = jnp.exp(lam_re * dt)
    ab_re = mag * jnp.cos(lam_im * dt)
    ab_im = mag * jnp.sin(lam_im * dt)
    den = lam_re * lam_re + lam_im * lam_im
    num_re = ab_re - 1.0
    f_re = (num_re * lam_re + ab_im * lam_im) / den
    f_im = (ab_im * lam_re - num_re * lam_im) / den
    b_re = b_re.astype(f32)
    b_im = b_im.astype(f32)
    bb_re = f_re[..., None] * b_re - f_im[..., None] * b_im
    bb_im = f_re[..., None] * b_im + f_im[..., None] * b_re
    bu_re = jnp.einsum("blgp,gnp->blgn", u, bb_re)
    bu_im = jnp.einsum("blgp,gnp->blgn", u, bb_im)
    a_full_re = jnp.broadcast_to(ab_re, bu_re.shape)
    a_full_im = jnp.broadcast_to(ab_im, bu_im.shape)

    def combine(e1, e2):
        a1r, a1i, b1r, b1i = e1
        a2r, a2i, b2r, b2i = e2
        return (a2r * a1r - a2i * a1i,
                a2r * a1i + a2i * a1r,
                a2r * b1r - a2i * b1i + b2r,
                a2r * b1i + a2i * b1r + b2i)

    _, _, s_re, s_im = lax.associative_scan(combine, (a_full_re, a_full_im, bu_re, bu_im), axis=1)
    y = (jnp.einsum("gpn,blgn->blgp", c_re.astype(f32), s_re)
         - jnp.einsum("gpn,blgn->blgp", c_im.astype(f32), s_im))
    return y + d_skip.astype(f32) * u


def hybrid_mixer(h, w_in, conv_w, conv_b, dt_bias, ssd_a_log, ssd_d, ssd_norm_gain,
                 s5_a_re, s5_a_im, s5_log_dt, s5_b_re, s5_b_im, s5_c_re, s5_c_im, s5_d,
                 w_glu, b_glu, s5_norm_gain, w_out):
    f32 = jnp.float32
    bsz, seq, _ = h.shape
    proj = h @ w_in
    z, xbc, dt_raw, u = jnp.split(proj, [SSD_WIDTH, SSD_WIDTH + CONV_DIM,
                                         SSD_WIDTH + CONV_DIM + SSD_HEADS], axis=-1)
    xbc = jax.nn.silu(causal_depthwise_conv(xbc, conv_w, conv_b))
    xs, bm, cm = jnp.split(xbc, [SSD_WIDTH, SSD_WIDTH + SSD_GROUPS * SSD_STATE], axis=-1)
    xs = xs.reshape(bsz, seq, SSD_HEADS, SSD_HEAD_DIM)
    bm = bm.reshape(bsz, seq, SSD_GROUPS, SSD_STATE).astype(f32)
    cm = cm.reshape(bsz, seq, SSD_GROUPS, SSD_STATE).astype(f32)
    dt = jax.nn.softplus((dt_raw + dt_bias).astype(f32))
    a = -jnp.exp(ssd_a_log.astype(f32))
    y = ssd_chunked(xs.astype(f32) * dt[..., None], dt * a, bm, cm)
    y = y + ssd_d.astype(f32)[:, None] * xs.astype(f32)
    y = y.reshape(bsz, seq, SSD_WIDTH).astype(h.dtype)
    y_ssd = rms_norm(y * jax.nn.silu(z), ssd_norm_gain)
    y5 = s5_scan(u.reshape(bsz, seq, S5_GROUPS, S5_GROUP_DIM), s5_a_re, s5_a_im, s5_log_dt,
                 s5_b_re, s5_b_im, s5_c_re, s5_c_im, s5_d.reshape(S5_GROUPS, S5_GROUP_DIM))
    g = jax.nn.gelu(y5.reshape(bsz, seq, S5_WIDTH).astype(h.dtype))
    y5 = g * jax.nn.sigmoid(g @ w_glu + b_glu)
    y_s5 = rms_norm(y5, s5_norm_gain)
    return jnp.concatenate([y_ssd, y_s5], axis=-1) @ w_out


def swiglu(x, w1, w3, w2):
    return (jax.nn.silu(x @ w1) * (x @ w3)) @ w2


def route(h, w_router, router_bias):
    f32 = jnp.float32
    n_tok = h.shape[0]
    scores = jax.nn.sigmoid((h @ w_router).astype(f32))
    biased = scores + router_bias.astype(f32)
    grp = biased.reshape(n_tok, N_EXPERT_GROUPS, N_EXPERTS // N_EXPERT_GROUPS)
    grp_score = lax.top_k(grp, 2)[0].sum(-1)
    _, gidx = lax.top_k(grp_score, TOPK_GROUPS)
    gmask = (gidx[..., None] == jnp.arange(N_EXPERT_GROUPS)).any(-2)
    masked = jnp.where(gmask[..., None], grp, -jnp.inf).reshape(n_tok, N_EXPERTS)
    _, eidx = lax.top_k(masked, TOP_K)
    w = jnp.take_along_axis(scores, eidx, axis=-1)
    w = w / jnp.sum(w, axis=-1, keepdims=True) * ROUTED_SCALE
    return eidx, w


def routed_experts(h, eidx, gate_w, w1, w3, w2):
    n_tok, d = h.shape
    n_assign = n_tok * TOP_K
    n_blocks = (n_assign + N_EXPERTS * (MOE_BLOCK - 1) + MOE_BLOCK - 1) // MOE_BLOCK
    flat_e = eidx.reshape(-1)
    flat_t = jnp.repeat(jnp.arange(n_tok, dtype=jnp.int32), TOP_K)
    flat_w = gate_w.reshape(-1)
    order = jnp.argsort(flat_e)
    se, st, sw = flat_e[order], flat_t[order], flat_w[order]
    counts = jnp.bincount(flat_e, length=N_EXPERTS)
    padded = (counts + MOE_BLOCK - 1) // MOE_BLOCK * MOE_BLOCK
    pad_end = jnp.cumsum(padded)
    pad_start = pad_end - padded
    start = jnp.cumsum(counts) - counts
    dest = pad_start[se] + jnp.arange(n_assign, dtype=jnp.int32) - start[se]
    tok_buf = jnp.full((n_blocks * MOE_BLOCK,), n_tok, jnp.int32).at[dest].set(st)
    w_buf = jnp.zeros((n_blocks * MOE_BLOCK,), sw.dtype).at[dest].set(sw)
    block_e = jnp.minimum(jnp.searchsorted(pad_end, jnp.arange(n_blocks, dtype=jnp.int32) * MOE_BLOCK,
                                           side="right"), N_EXPERTS - 1)
    h_pad = jnp.concatenate([h, jnp.zeros((1, d), h.dtype)], axis=0)

    def expert_block(args):
        tok, wt, e = args
        xb = h_pad[tok]
        return swiglu(xb, w1[e], w3[e], w2[e]) * wt[:, None].astype(h.dtype)

    out = lax.map(expert_block, (tok_buf.reshape(n_blocks, MOE_BLOCK),
                                 w_buf.reshape(n_blocks, MOE_BLOCK), block_e))
    return jax.ops.segment_sum(out.reshape(-1, d), tok_buf, num_segments=n_tok + 1)[:n_tok]


def setup_inputs(seed: int = 0) -> dict:
    key = jax.random.key(seed)
    ks = jax.random.split(key, 40)
    f32 = jnp.float32

    def nrm(k, shape, scale):
        return jax.random.normal(k, shape, f32) * scale

    L = DEPTH
    dt0 = jnp.exp(jax.random.uniform(ks[6], (L, SSD_HEADS), f32, math.log(1e-3), math.log(1e-1)))
    n_idx = jnp.arange(S5_STATE, dtype=f32)
    return {
        "x": nrm(ks[0], (BATCH, SEQ, D_MODEL), 1.0),
        "c": nrm(ks[1], (BATCH, D_MODEL), 1.0),
        "w_ada": nrm(ks[2], (L, D_MODEL, N_MOD * D_MODEL), 0.5 * D_MODEL ** -0.5),
        "b_ada": nrm(ks[3], (L, N_MOD * D_MODEL), 0.02),
        "norm1_gain": 1.0 + nrm(ks[4], (L, D_MODEL), 0.02),
        "w_in": nrm(ks[5], (L, D_MODEL, IN_PROJ), D_MODEL ** -0.5),
        "conv_w": nrm(ks[7], (L, SSD_CONV, CONV_DIM), SSD_CONV ** -0.5),
        "conv_b": nrm(ks[8], (L, CONV_DIM), 0.02),
        "dt_bias": dt0 + jnp.log(-jnp.expm1(-dt0)),
        "ssd_a_log": jnp.log(jax.random.uniform(ks[9], (L, SSD_HEADS), f32, 1.0, 16.0)),
        "ssd_d": 1.0 + nrm(ks[10], (L, SSD_HEADS), 0.1),
        "ssd_norm_gain": 1.0 + nrm(ks[11], (L, SSD_WIDTH), 0.02),
        "s5_a_re": -0.5 + nrm(ks[12], (L, S5_GROUPS, S5_STATE), 0.01),
        "s5_a_im": math.pi * n_idx + nrm(ks[13], (L, S5_GROUPS, S5_STATE), 0.01),
        "s5_log_dt": jax.random.uniform(ks[14], (L, S5_GROUPS), f32, math.log(1e-3), math.log(1e-1)),
        "s5_b_re": nrm(ks[15], (L, S5_GROUPS, S5_STATE, S5_GROUP_DIM), (2 * S5_GROUP_DIM) ** -0.5),
        "s5_b_im": nrm(ks[16], (L, S5_GROUPS, S5_STATE, S5_GROUP_DIM), (2 * S5_GROUP_DIM) ** -0.5),
        "s5_c_re": nrm(ks[17], (L, S5_GROUPS, S5_GROUP_DIM, S5_STATE), (2 * S5_STATE) ** -0.5),
        "s5_c_im": nrm(ks[18], (L, S5_GROUPS, S5_GROUP_DIM, S5_STATE), (2 * S5_STATE) ** -0.5),
        "s5_d": nrm(ks[19], (L, S5_WIDTH), 1.0),
        "w_glu": nrm(ks[20], (L, S5_WIDTH, S5_WIDTH), S5_WIDTH ** -0.5),
        "b_glu": nrm(ks[21], (L, S5_WIDTH), 0.02),
        "s5_norm_gain": 1.0 + nrm(ks[22], (L, S5_WIDTH), 0.02),
        "w_out": nrm(ks[23], (L, MIX_WIDTH, D_MODEL), MIX_WIDTH ** -0.5),
        "norm2_gain": 1.0 + nrm(ks[24], (L, D_MODEL), 0.02),
        "w_router": nrm(ks[25], (L, D_MODEL, N_EXPERTS), D_MODEL ** -0.5),
        "router_bias": nrm(ks[26], (L, N_EXPERTS), 0.01),
        "w1": nrm(ks[27], (L, N_EXPERTS, D_MODEL, EXPERT_FF), D_MODEL ** -0.5),
        "w3": nrm(ks[28], (L, N_EXPERTS, D_MODEL, EXPERT_FF), D_MODEL ** -0.5),
        "w2": nrm(ks[29], (L, N_EXPERTS, EXPERT_FF, D_MODEL), EXPERT_FF ** -0.5),
        "ws1": nrm(ks[30], (L, D_MODEL, SHARED_FF), D_MODEL ** -0.5),
        "ws3": nrm(ks[31], (L, D_MODEL, SHARED_FF), D_MODEL ** -0.5),
        "ws2": nrm(ks[32], (L, SHARED_FF, D_MODEL), SHARED_FF ** -0.5),
        "final_gain": 1.0 + nrm(ks[33], (D_MODEL,), 0.02),
    }


def reference(x, c, w_ada, b_ada, norm1_gain, w_in, conv_w, conv_b, dt_bias, ssd_a_log, ssd_d,
              ssd_norm_gain, s5_a_re, s5_a_im, s5_log_dt, s5_b_re, s5_b_im, s5_c_re, s5_c_im, s5_d,
              w_glu, b_glu, s5_norm_gain, w_out, norm2_gain, w_router, router_bias, w1, w3, w2,
              ws1, ws3, ws2, final_gain):
    bsz, seq, d = x.shape
    c_act = jax.nn.silu(c)
    for i in range(DEPTH):
        mod = c_act @ w_ada[i] + b_ada[i]
        sh1, sc1, g1, sh2, sc2, g2 = jnp.split(mod[:, None, :], N_MOD, axis=-1)
        h = rms_norm(x, norm1_gain[i]) * (1.0 + sc1) + sh1
        mix = hybrid_mixer(h, w_in[i], conv_w[i], conv_b[i], dt_bias[i], ssd_a_log[i], ssd_d[i],
                           ssd_norm_gain[i], s5_a_re[i], s5_a_im[i], s5_log_dt[i], s5_b_re[i],
                           s5_b_im[i], s5_c_re[i], s5_c_im[i], s5_d[i], w_glu[i], b_glu[i],
                           s5_norm_gain[i], w_out[i])
        x = x + g1 * mix
        h = rms_norm(x, norm2_gain[i]) * (1.0 + sc2) + sh2
        hf = h.reshape(bsz * seq, d)
        eidx, gate_w = route(hf, w_router[i], router_bias[i])
        y = swiglu(hf, ws1[i], ws3[i], ws2[i]) + routed_experts(hf, eidx, gate_w, w1[i], w3[i], w2[i])
        x = x + g2 * y.reshape(bsz, seq, d)
    return rms_norm(x, final_gain)
```

```python
import functools
import math

import jax
import jax.numpy as jnp
from jax import lax
from jax.experimental import pallas as pl
from jax.experimental.pallas import tpu as pltpu

F32 = jnp.float32
BF16 = jnp.bfloat16
I32 = jnp.int32
HIGHEST = lax.Precision.HIGHEST

D_MODEL = 1024
SSD_HEADS = 16
SSD_HEAD_DIM = 64
SSD_WIDTH = SSD_HEADS * SSD_HEAD_DIM
SSD_GROUPS = 4
SSD_HPG = SSD_HEADS // SSD_GROUPS
SSD_STATE = 128
SSD_CONV = 4
S5_GROUP_DIM = 16
S5_GROUPS = 32
S5_WIDTH = S5_GROUPS * S5_GROUP_DIM
S5_STATE = 64
S5_LANES = S5_GROUPS * S5_STATE
S5_SLABS = 4
BC_WIDTH = SSD_GROUPS * SSD_STATE
CONV_DIM = SSD_WIDTH + 2 * BC_WIDTH
N_EXPERTS = 256
TOP_K = 8
N_EXPERT_GROUPS = 8
EXPERTS_PER_GROUP = N_EXPERTS // N_EXPERT_GROUPS
TOPK_GROUPS = 4
EXPERT_FF = 256
SHARED_FF = 256
ROUTED_SCALE = 2.5
N_MOD = 6
EPS = 1e-6

SUBLANES = 8
SSD_CHUNK = 256
TOKEN_TILE = 512
S5_TILE = 256
ROUTE_TILE = 512
MOE_TILE = 256
EXPERT_BLOCK = 256
VMEM_LIMIT = 56 * 1024 * 1024

NT_DIMS = (((1,), (1,)), ((), ()))


def _nt(a, b):
    return lax.dot_general(a, b, NT_DIMS, preferred_element_type=F32)


def _dot(a, b):
    return jnp.dot(a, b, preferred_element_type=F32)


def _dot_f32(a, b):
    return jnp.dot(a, b, precision=HIGHEST, preferred_element_type=F32)


def _silu(x):
    return x * jax.nn.sigmoid(x)


def _softplus(x):
    return jnp.maximum(x, 0.0) + jnp.log1p(jnp.exp(-jnp.abs(x)))


def _rms(x, gain):
    return x * lax.rsqrt(jnp.mean(x * x, axis=-1, keepdims=True) + EPS) * gain


def _params(n_grid_dims):
    return pltpu.CompilerParams(dimension_semantics=("arbitrary",) * n_grid_dims,
                                vmem_limit_bytes=VMEM_LIMIT)


def _full(shape):
    zeros = (0,) * len(shape)
    return pl.BlockSpec(shape, lambda *_: zeros)


def _mod_kernel(c_ref, w_ref, b_ref, o_ref):
    o_ref[...] = _dot_f32(_silu(c_ref[...]), w_ref[...]) + b_ref[...]


def _modulation(c, w_ada, b_ada):
    bsz = c.shape[0]
    rows = -(-bsz // SUBLANES) * SUBLANES
    c_pad = jnp.pad(c, ((0, rows - bsz), (0, 0)))
    n_out = w_ada.shape[1]
    tile = 512
    out = pl.pallas_call(
        _mod_kernel,
        grid=(n_out // tile,),
        in_specs=[_full((rows, D_MODEL)),
                  pl.BlockSpec((D_MODEL, tile), lambda j: (0, j)),
                  pl.BlockSpec((1, tile), lambda j: (0, j))],
        out_specs=pl.BlockSpec((rows, tile), lambda j: (0, j)),
        out_shape=jax.ShapeDtypeStruct((rows, n_out), F32),
        compiler_params=_params(1),
        name="mod",
    )(c_pad, w_ada, b_ada.reshape(1, n_out))
    return out[:bsz]


def _inproj_kernel(x_ref, sc_ref, sh_ref, g_ref, wz_ref, wxbc_ref, wdt_ref, wdtT_ref, wu_ref,
                   z_ref, xbc_ref, dt_ref, dtT_ref, u_ref):
    h = _rms(x_ref[0], g_ref[...]) * (1.0 + sc_ref[0]) + sh_ref[0]
    hb = h.astype(BF16)
    z_ref[0] = _dot(hb, wz_ref[...]).astype(BF16)
    xbc_ref[0] = _dot(hb, wxbc_ref[...]).astype(BF16)
    dt_ref[0] = _dot(hb, wdt_ref[...])
    dtT_ref[0] = _nt(wdtT_ref[...], hb)
    u_ref[0] = _dot(hb, wu_ref[...])


def _inproj(x, sc, sh, gain, w_in):
    bsz, seq, _ = x.shape
    tm = min(TOKEN_TILE, seq)
    o0, o1, o2 = SSD_WIDTH, SSD_WIDTH + CONV_DIM, SSD_WIDTH + CONV_DIM + SSD_HEADS
    wz = w_in[:, :o0].astype(BF16)
    wxbc = w_in[:, o0:o1].astype(BF16)
    wdt = w_in[:, o1:o2].astype(BF16)
    wu = w_in[:, o2:].astype(BF16)
    tok = lambda n: pl.BlockSpec((1, tm, n), lambda b, i: (b, i, 0))
    per_b = pl.BlockSpec((1, 1, D_MODEL), lambda b, i: (b, 0, 0))
    return pl.pallas_call(
        _inproj_kernel,
        grid=(bsz, seq // tm),
        in_specs=[tok(D_MODEL), per_b, per_b, _full((1, D_MODEL)),
                  _full(wz.shape), _full(wxbc.shape), _full(wdt.shape), _full((SSD_HEADS, D_MODEL)),
                  _full(wu.shape)],
        out_specs=[tok(SSD_WIDTH), tok(CONV_DIM), tok(SSD_HEADS),
                   pl.BlockSpec((1, SSD_HEADS, tm), lambda b, i: (b, 0, i)), tok(S5_WIDTH)],
        out_shape=[jax.ShapeDtypeStruct((bsz, seq, SSD_WIDTH), BF16),
                   jax.ShapeDtypeStruct((bsz, seq, CONV_DIM), BF16),
                   jax.ShapeDtypeStruct((bsz, seq, SSD_HEADS), F32),
                   jax.ShapeDtypeStruct((bsz, SSD_HEADS, seq), F32),
                   jax.ShapeDtypeStruct((bsz, seq, S5_WIDTH), F32)],
        compiler_params=_params(2),
        name="inproj",
    )(x, sc[:, None, :], sh[:, None, :], gain.reshape(1, D_MODEL), wz, wxbc, wdt, wdt.T, wu)


def _ssd_kernel(xbc_ref, z_ref, dt_ref, dtT_ref, cw_ref, cb_ref, dtb_ref, dtbT_ref, alog_ref,
                alogT_ref, dskip_ref, ngain_ref, y_ref, xp_ref, state_ref, yacc_ref):
    q = xbc_ref.shape[1]
    first = pl.program_id(1) == 0

    @pl.when(first)
    def _():
        xp_ref[0:SUBLANES, :] = jnp.zeros((SUBLANES, CONV_DIM), F32)
        state_ref[...] = jnp.zeros_like(state_ref)

    @pl.when(jnp.logical_not(first))
    def _():
        xp_ref[0:SUBLANES, :] = xp_ref[q:q + SUBLANES, :]

    xp_ref[SUBLANES:, :] = xbc_ref[0].astype(F32)
    acc = cb_ref[...] + cw_ref[SSD_CONV - 1:SSD_CONV, :] * xp_ref[SUBLANES:, :]
    for j in range(1, SSD_CONV):
        acc = acc + cw_ref[SSD_CONV - 1 - j:SSD_CONV - j, :] * xp_ref[SUBLANES - j:SUBLANES - j + q, :]
    act = _silu(acc)
    xs = act[:, :SSD_WIDTH]
    bm = act[:, SSD_WIDTH:SSD_WIDTH + BC_WIDTH]
    cm = act[:, SSD_WIDTH + BC_WIDTH:]

    dt = _softplus(dt_ref[0] + dtb_ref[...])
    dtT = _softplus(dtT_ref[0] + dtbT_ref[...])
    a = dt * (-jnp.exp(alog_ref[...]))
    aT = dtT * (-jnp.exp(alogT_ref[...]))
    ri = lax.broadcasted_iota(I32, (q, q), 0)
    ci = lax.broadcasted_iota(I32, (q, q), 1)
    causal = ci <= ri
    acs = _dot_f32(jnp.where(causal, 1.0, 0.0), a)
    acsT = _dot_f32(aT, jnp.where(ri <= ci, 1.0, 0.0))
    acs_last = acs[q - 1:q, :]
    to_end = jnp.exp(acs_last - acs)
    from_start = jnp.exp(acs)
    total = jnp.exp(acs_last)

    for g in range(SSD_GROUPS):
        bg = bm[:, g * SSD_STATE:(g + 1) * SSD_STATE]
        cgb = cm[:, g * SSD_STATE:(g + 1) * SSD_STATE].astype(BF16)
        scores = _nt(cgb, bg.astype(BF16))
        bgT = bg.T.astype(BF16)
        prev = state_ref[g]
        y_off = _dot(cgb, prev.astype(BF16))
        for hh in range(SSD_HPG):
            h = g * SSD_HPG + hh
            lo, hi = hh * SSD_HEAD_DIM, (hh + 1) * SSD_HEAD_DIM
            seg = acs[:, h:h + 1] - acsT[h:h + 1, :]
            decay = jnp.exp(jnp.where(causal, seg, -jnp.inf))
            xh = xs[:, h * SSD_HEAD_DIM:(h + 1) * SSD_HEAD_DIM]
            xdt = xh * dt[:, h:h + 1]
            y_diag = _dot((scores * decay).astype(BF16), xdt.astype(BF16))
            y_h = y_diag + y_off[:, lo:hi] * from_start[:, h:h + 1] + dskip_ref[:, h:h + 1] * xh
            yacc_ref[:, h * SSD_HEAD_DIM:(h + 1) * SSD_HEAD_DIM] = y_h
            new = _dot(bgT, (xdt * to_end[:, h:h + 1]).astype(BF16))
            state_ref[g, :, lo:hi] = prev[:, lo:hi] * total[:, h:h + 1] + new

    z = z_ref[0].astype(F32)
    y_ref[0] = _rms(yacc_ref[...] * _silu(z), ngain_ref[...]).astype(BF16)


def _ssd(xbc, z, dt_raw, dtT_raw, conv_w, conv_b, dt_bias, a_log, d_skip, norm_gain):
    bsz, seq, _ = xbc.shape
    q = min(SSD_CHUNK, seq)
    tok = lambda n: pl.BlockSpec((1, q, n), lambda b, c: (b, c, 0))
    row = lambda v: v.reshape(1, -1)
    col = lambda v: v.reshape(-1, 1)
    return pl.pallas_call(
        _ssd_kernel,
        grid=(bsz, seq // q),
        in_specs=[tok(CONV_DIM), tok(SSD_WIDTH), tok(SSD_HEADS),
                  pl.BlockSpec((1, SSD_HEADS, q), lambda b, c: (b, 0, c)),
                  _full((SSD_CONV, CONV_DIM)), _full((1, CONV_DIM)),
                  _full((1, SSD_HEADS)), _full((SSD_HEADS, 1)),
                  _full((1, SSD_HEADS)), _full((SSD_HEADS, 1)),
                  _full((1, SSD_HEADS)), _full((1, SSD_WIDTH))],
        out_specs=tok(SSD_WIDTH),
        out_shape=jax.ShapeDtypeStruct((bsz, seq, SSD_WIDTH), BF16),
        scratch_shapes=[pltpu.VMEM((q + SUBLANES, CONV_DIM), F32),
                        pltpu.VMEM((SSD_GROUPS, SSD_STATE, SSD_HPG * SSD_HEAD_DIM), F32),
                        pltpu.VMEM((q, SSD_WIDTH), F32)],
        compiler_params=_params(2),
        name="ssd",
    )(xbc, z, dt_raw, dtT_raw, conv_w, row(conv_b), row(dt_bias), col(dt_bias), row(a_log),
      col(a_log), row(d_skip), row(norm_gain))


def _s5_param_kernel(are_ref, aim_ref, ldt_ref, bre_ref, bim_ref, pw_re_ref, pw_im_ref,
                     bb_re_ref, bb_im_ref):
    dt = jnp.exp(ldt_ref[...])
    lam_re = jnp.minimum(are_ref[...], -1e-4)
    lam_im = aim_ref[...]
    mag = jnp.exp(lam_re * dt)
    ab_re = mag * jnp.cos(lam_im * dt)
    ab_im = mag * jnp.sin(lam_im * dt)
    den = lam_re * lam_re + lam_im * lam_im
    num_re = ab_re - 1.0
    f_re = (num_re * lam_re + ab_im * lam_im) / den
    f_im = (ab_im * lam_re - num_re * lam_im) / den
    bre = bre_ref[...]
    bim = bim_ref[...]
    bb_re_ref[...] = f_re[:, None, :] * bre - f_im[:, None, :] * bim
    bb_im_ref[...] = f_re[:, None, :] * bim + f_im[:, None, :] * bre
    p_re, p_im = ab_re, ab_im
    pw_re_ref[0] = p_re
    pw_im_ref[0] = p_im
    for k in range(1, SUBLANES):
        p_re, p_im = p_re * ab_re - p_im * ab_im, p_re * ab_im + p_im * ab_re
        pw_re_ref[k] = p_re
        pw_im_ref[k] = p_im


def _s5_params(a_re, a_im, log_dt, b_re, b_im):
    g, n, p = S5_GROUPS, S5_STATE, S5_GROUP_DIM
    return pl.pallas_call(
        _s5_param_kernel,
        grid=(1,),
        in_specs=[_full((g, n)), _full((g, n)), _full((g, 1)), _full((g, p, n)), _full((g, p, n))],
        out_specs=[_full((SUBLANES, g, n)), _full((SUBLANES, g, n)), _full((g, p, n)), _full((g, p, n))],
        out_shape=[jax.ShapeDtypeStruct((SUBLANES, g, n), F32)] * 2
        + [jax.ShapeDtypeStruct((g, p, n), F32)] * 2,
        compiler_params=_params(1),
        name="s5prm",
    )(a_re, a_im, log_dt.reshape(g, 1), b_re.transpose(0, 2, 1), b_im.transpose(0, 2, 1))


def _block_diag(w):
    _, r, c = w.shape
    per = S5_GROUPS // S5_SLABS
    eye = jnp.eye(per, dtype=w.dtype)
    w5 = w.reshape(S5_SLABS, per, r, c)
    return (w5[:, :, :, None, :] * eye[None, :, None, :, None]).reshape(S5_SLABS, per * r, per * c)


SCAN_LEVELS = (1, 2, 4)


def _s5_kernel(u_ref, wb_re_ref, wb_im_ref, lev_re_ref, lev_im_ref, pw_re_ref, pw_im_ref,
               wc_re_ref, wc_im_ref, dsk_ref, wglu_ref, bglu_ref, gain_ref,
               y_ref, sre_ref, sim_ref, car_ref):
    tt = u_ref.shape[1]
    slab_u = S5_WIDTH // S5_SLABS
    slab_s = S5_LANES // S5_SLABS

    @pl.when(pl.program_id(1) == 0)
    def _():
        car_ref[...] = jnp.zeros_like(car_ref)

    u = u_ref[0]
    ub = u.astype(BF16)
    for j in range(S5_SLABS):
        uj = ub[:, j * slab_u:(j + 1) * slab_u]
        sre_ref[:, j * slab_s:(j + 1) * slab_s] = _dot(uj, wb_re_ref[j])
        sim_ref[:, j * slab_s:(j + 1) * slab_s] = _dot(uj, wb_im_ref[j])

    def step(i, carry):
        c_re, c_im = carry
        r0 = pl.multiple_of(i * SUBLANES, SUBLANES)
        x_re = sre_ref[pl.ds(r0, SUBLANES), :]
        x_im = sim_ref[pl.ds(r0, SUBLANES), :]
        for li, d in enumerate(SCAN_LEVELS):
            a_re, a_im = lev_re_ref[li], lev_im_ref[li]
            s_re = pltpu.roll(x_re, d, axis=0)
            s_im = pltpu.roll(x_im, d, axis=0)
            x_re, x_im = x_re + a_re * s_re - a_im * s_im, x_im + a_re * s_im + a_im * s_re
        p_re, p_im = pw_re_ref[...], pw_im_ref[...]
        x_re, x_im = x_re + p_re * c_re - p_im * c_im, x_im + p_re * c_im + p_im * c_re
        sre_ref[pl.ds(r0, SUBLANES), :] = x_re
        sim_ref[pl.ds(r0, SUBLANES), :] = x_im
        return x_re[SUBLANES - 1:, :], x_im[SUBLANES - 1:, :]

    c_re, c_im = lax.fori_loop(0, tt // SUBLANES, step, (car_ref[0:1, :], car_ref[1:2, :]))
    car_ref[0:1, :] = c_re
    car_ref[1:2, :] = c_im

    parts = []
    for j in range(S5_SLABS):
        s_re = sre_ref[:, j * slab_s:(j + 1) * slab_s].astype(BF16)
        s_im = sim_ref[:, j * slab_s:(j + 1) * slab_s].astype(BF16)
        parts.append(_dot(s_re, wc_re_ref[j]) - _dot(s_im, wc_im_ref[j]))
    y5 = jnp.concatenate(parts, axis=-1) + dsk_ref[...] * u
    g = jax.nn.gelu(y5)
    gated = g * jax.nn.sigmoid(_dot(g.astype(BF16), wglu_ref[...]) + bglu_ref[...])
    y_ref[0] = _rms(gated, gain_ref[...]).astype(BF16)


def _s5(u, a_re, a_im, log_dt, b_re, b_im, c_re, c_im, d_skip, w_glu, b_glu, norm_gain):
    bsz, seq, _ = u.shape
    tt = min(S5_TILE, seq)
    pw_re, pw_im, bb_re, bb_im = _s5_params(a_re, a_im, log_dt, b_re, b_im)
    wb_re = _block_diag(bb_re).astype(BF16)
    wb_im = _block_diag(bb_im).astype(BF16)
    wc_re = _block_diag(c_re.transpose(0, 2, 1)).astype(BF16)
    wc_im = _block_diag(c_im.transpose(0, 2, 1)).astype(BF16)
    pw_re = pw_re.reshape(SUBLANES, S5_LANES)
    pw_im = pw_im.reshape(SUBLANES, S5_LANES)
    rows = jnp.arange(SUBLANES)[:, None]
    lev_re = jnp.stack([jnp.where(rows >= d, pw_re[d - 1][None, :], 0.0) for d in SCAN_LEVELS])
    lev_im = jnp.stack([jnp.where(rows >= d, pw_im[d - 1][None, :], 0.0) for d in SCAN_LEVELS])
    tok = pl.BlockSpec((1, tt, S5_WIDTH), lambda b, i: (b, i, 0))
    return pl.pallas_call(
        _s5_kernel,
        grid=(bsz, seq // tt),
        in_specs=[tok, _full(wb_re.shape), _full(wb_im.shape), _full(lev_re.shape), _full(lev_im.shape),
                  _full(pw_re.shape), _full(pw_im.shape), _full(wc_re.shape), _full(wc_im.shape),
                  _full((1, S5_WIDTH)), _full((S5_WIDTH, S5_WIDTH)), _full((1, S5_WIDTH)),
                  _full((1, S5_WIDTH))],
        out_specs=tok,
        out_shape=jax.ShapeDtypeStruct((bsz, seq, S5_WIDTH), BF16),
        scratch_shapes=[pltpu.VMEM((tt, S5_LANES), F32), pltpu.VMEM((tt, S5_LANES), F32),
                        pltpu.VMEM((SUBLANES, S5_LANES), F32)],
        compiler_params=_params(2),
        name="s5",
    )(u, wb_re, wb_im, lev_re, lev_im, pw_re, pw_im, wc_re, wc_im, d_skip.reshape(1, S5_WIDTH),
      w_glu.astype(BF16), b_glu.reshape(1, S5_WIDTH), norm_gain.reshape(1, S5_WIDTH))


def _outproj_kernel(x_ref, ys_ref, y5_ref, g1_ref, w1_ref, w2_ref, o_ref):
    mix = _dot(ys_ref[0], w1_ref[...]) + _dot(y5_ref[0], w2_ref[...])
    o_ref[0] = x_ref[0] + g1_ref[0] * mix


def _outproj(x, y_ssd, y_s5, g1, w_out):
    bsz, seq, _ = x.shape
    tm = min(TOKEN_TILE, seq)
    w1 = w_out[:SSD_WIDTH].astype(BF16)
    w2 = w_out[SSD_WIDTH:].astype(BF16)
    tok = lambda n: pl.BlockSpec((1, tm, n), lambda b, i: (b, i, 0))
    return pl.pallas_call(
        _outproj_kernel,
        grid=(bsz, seq // tm),
        in_specs=[tok(D_MODEL), tok(SSD_WIDTH), tok(S5_WIDTH),
                  pl.BlockSpec((1, 1, D_MODEL), lambda b, i: (b, 0, 0)), _full(w1.shape), _full(w2.shape)],
        out_specs=tok(D_MODEL),
        out_shape=jax.ShapeDtypeStruct((bsz, seq, D_MODEL), F32),
        compiler_params=_params(2),
        name="outproj",
    )(x, y_ssd, y_s5, g1[:, None, :], w1, w2)


def _first_argmax(v, idx, sentinel):
    m = jnp.max(v, axis=0, keepdims=True)
    return m, jnp.min(jnp.where(v == m, idx, sentinel), axis=0, keepdims=True)


def _router_kernel(x_ref, sc_ref, sh_ref, g_ref, whi_ref, wlo_ref, rb_ref,
                   eidx_ref, gw_ref, pos_ref, cnt_ref, run_ref):
    tm = x_ref.shape[0]

    @pl.when(pl.program_id(0) == 0)
    def _():
        run_ref[...] = jnp.zeros_like(run_ref)

    h = _rms(x_ref[...], g_ref[...]) * (1.0 + sc_ref[0]) + sh_ref[0]
    h_hi = h.astype(BF16)
    h_lo = (h - h_hi.astype(F32)).astype(BF16)
    logits = _nt(whi_ref[...], h_hi) + _nt(whi_ref[...], h_lo) + _nt(wlo_ref[...], h_hi)
    scores = jax.nn.sigmoid(logits)
    biased = scores + rb_ref[...]
    rows = lax.broadcasted_iota(I32, (N_EXPERTS, tm), 0)
    neg = -jnp.inf

    group_scores = []
    local = lax.broadcasted_iota(I32, (EXPERTS_PER_GROUP, tm), 0)
    for g in range(N_EXPERT_GROUPS):
        blk = biased[g * EXPERTS_PER_GROUP:(g + 1) * EXPERTS_PER_GROUP]
        m1, i1 = _first_argmax(blk, local, EXPERTS_PER_GROUP)
        m2 = jnp.max(jnp.where(local == i1, neg, blk), axis=0, keepdims=True)
        group_scores.append(m1 + m2)
    cur = jnp.concatenate(group_scores, axis=0)
    grow = lax.broadcasted_iota(I32, (N_EXPERT_GROUPS, tm), 0)
    chosen = jnp.zeros((N_EXPERT_GROUPS, tm), F32)
    for _ in range(TOPK_GROUPS):
        _, gi = _first_argmax(cur, grow, N_EXPERT_GROUPS)
        hit = grow == gi
        chosen = jnp.where(hit, 1.0, chosen)
        cur = jnp.where(hit, neg, cur)
    cur = jnp.concatenate(
        [jnp.where(chosen[g:g + 1] > 0.0, biased[g * EXPERTS_PER_GROUP:(g + 1) * EXPERTS_PER_GROUP], neg)
         for g in range(N_EXPERT_GROUPS)], axis=0)

    picked = jnp.zeros((N_EXPERTS, tm), F32)
    idxs, gates = [], []
    for _ in range(TOP_K):
        _, ei = _first_argmax(cur, rows, N_EXPERTS)
        hit = rows == ei
        gates.append(jnp.sum(jnp.where(hit, scores, 0.0), axis=0, keepdims=True))
        idxs.append(ei)
        picked = jnp.where(hit, 1.0, picked)
        cur = jnp.where(hit, neg, cur)
    gate = jnp.concatenate(gates, axis=0)
    gw_ref[...] = gate / jnp.sum(gate, axis=0, keepdims=True) * ROUTED_SCALE
    eidx_ref[...] = jnp.concatenate(idxs, axis=0)

    t_from = lax.broadcasted_iota(I32, (tm, tm), 0)
    t_to = lax.broadcasted_iota(I32, (tm, tm), 1)
    earlier = jnp.where(t_from < t_to, 1.0, 0.0).astype(BF16)
    before = _dot(picked.astype(BF16), earlier) + run_ref[...]
    pos_ref[...] = jnp.concatenate(
        [jnp.sum(jnp.where(rows == ei, before, 0.0), axis=0, keepdims=True) for ei in idxs], axis=0).astype(I32)
    run_ref[...] = run_ref[...] + jnp.sum(picked, axis=1, keepdims=True)
    cnt_ref[...] = run_ref[...]


def _router(x2, sc, sh, gain, w_router, router_bias, seq):
    n_tok = x2.shape[0]
    tm = min(ROUTE_TILE, seq)
    per_seq = seq // tm
    wT = w_router.T
    w_hi = wT.astype(BF16)
    w_lo = (wT - w_hi.astype(F32)).astype(BF16)
    per_b = pl.BlockSpec((1, 1, D_MODEL), lambda i: (i // per_seq, 0, 0))
    kt = pl.BlockSpec((TOP_K, tm), lambda i: (0, i))
    return pl.pallas_call(
        _router_kernel,
        grid=(n_tok // tm,),
        in_specs=[pl.BlockSpec((tm, D_MODEL), lambda i: (i, 0)), per_b, per_b, _full((1, D_MODEL)),
                  _full((N_EXPERTS, D_MODEL)), _full((N_EXPERTS, D_MODEL)), _full((N_EXPERTS, 1))],
        out_specs=[kt, kt, kt, _full((N_EXPERTS, 1))],
        out_shape=[jax.ShapeDtypeStruct((TOP_K, n_tok), I32), jax.ShapeDtypeStruct((TOP_K, n_tok), F32),
                   jax.ShapeDtypeStruct((TOP_K, n_tok), I32), jax.ShapeDtypeStruct((N_EXPERTS, 1), F32)],
        scratch_shapes=[pltpu.VMEM((N_EXPERTS, 1), F32)],
        compiler_params=_params(1),
        name="router",
    )(x2, sc[:, None, :], sh[:, None, :], gain.reshape(1, D_MODEL), w_hi, w_lo,
      router_bias.reshape(N_EXPERTS, 1))


def _dest_kernel(eidx_ref, pos_ref, start_ref, dest_ref):
    tm = eidx_ref.shape[1]
    rows = lax.broadcasted_iota(I32, (N_EXPERTS, tm), 0)
    start = start_ref[...]
    dest_ref[...] = pos_ref[...] + jnp.concatenate(
        [jnp.sum(jnp.where(rows == eidx_ref[k:k + 1, :], start, 0), axis=0, keepdims=True)
         for k in range(TOP_K)], axis=0)


def _dest_rows(eidx, pos, pad_start):
    n_tok = eidx.shape[1]
    tm = min(2048, n_tok)
    kt = pl.BlockSpec((TOP_K, tm), lambda i: (0, i))
    return pl.pallas_call(
        _dest_kernel,
        grid=(n_tok // tm,),
        in_specs=[kt, kt, _full((N_EXPERTS, 1))],
        out_specs=kt,
        out_shape=jax.ShapeDtypeStruct((TOP_K, n_tok), I32),
        compiler_params=_params(1),
        name="dest",
    )(eidx, pos, pad_start.reshape(N_EXPERTS, 1))


def _row_copy(src_ref, src_row, dst_ref, dst_row, sem):
    return pltpu.make_async_copy(src_ref.at[pl.ds(src_row, 1)], dst_ref.at[pl.ds(dst_row, 1)], sem)


def _dispatch_kernel(dest_ref, x_ref, sc_ref, sh_ref, g_ref, buf_in_ref, buf_ref, h_ref, sem):
    del buf_in_ref
    tm = x_ref.shape[0]
    h_ref[...] = _rms(x_ref[...], g_ref[...]) * (1.0 + sc_ref[0]) + sh_ref[0]

    def start(t, _):
        for k in range(TOP_K):
            _row_copy(h_ref, t, buf_ref, dest_ref[k, t], sem).start()
        return 0

    def wait(t, _):
        for k in range(TOP_K):
            _row_copy(h_ref, t, buf_ref, dest_ref[k, t], sem).wait()
        return 0

    lax.fori_loop(0, tm, start, 0)
    lax.fori_loop(0, tm, wait, 0)


def _dispatch(x2, sc, sh, gain, dest, n_rows, seq):
    n_tok = x2.shape[0]
    tm = min(MOE_TILE, seq)
    per_seq = seq // tm
    per_b = pl.BlockSpec((1, 1, D_MODEL), lambda i: (i // per_seq, 0, 0))
    return pl.pallas_call(
        _dispatch_kernel,
        grid=(n_tok // tm,),
        in_specs=[pl.BlockSpec((TOP_K, tm), lambda i: (0, i), memory_space=pltpu.SMEM),
                  pl.BlockSpec((tm, D_MODEL), lambda i: (i, 0)), per_b, per_b, _full((1, D_MODEL)),
                  pl.BlockSpec(memory_space=pl.ANY)],
        out_specs=pl.BlockSpec(memory_space=pl.ANY),
        out_shape=jax.ShapeDtypeStruct((n_rows, D_MODEL), F32),
        scratch_shapes=[pltpu.VMEM((tm, D_MODEL), F32), pltpu.SemaphoreType.DMA(())],
        input_output_aliases={5: 0},
        compiler_params=_params(1),
        name="dispatch",
    )(dest, x2, sc[:, None, :], sh[:, None, :], gain.reshape(1, D_MODEL),
      jnp.zeros((n_rows, D_MODEL), F32))


def _experts_kernel(be_ref, na_ref, x_ref, w1_ref, w3_ref, w2_ref, y_ref):
    del be_ref

    @pl.when(pl.program_id(0) < na_ref[0])
    def _():
        xb = x_ref[...].astype(BF16)
        h1 = _dot(xb, w1_ref[0].astype(BF16))
        h3 = _dot(xb, w3_ref[0].astype(BF16))
        y_ref[...] = _dot((_silu(h1) * h3).astype(BF16), w2_ref[0].astype(BF16))


def _experts(x_sorted, block_expert, n_active, w1, w3, w2):
    n_rows = x_sorted.shape[0]
    n_blocks = n_rows // EXPERT_BLOCK
    rows = pl.BlockSpec((EXPERT_BLOCK, D_MODEL), lambda i, be, na: (jnp.minimum(i, na[0] - 1), 0))
    grid_spec = pltpu.PrefetchScalarGridSpec(
        num_scalar_prefetch=2,
        grid=(n_blocks,),
        in_specs=[rows,
                  pl.BlockSpec((1, D_MODEL, EXPERT_FF), lambda i, be, na: (be[i], 0, 0)),
                  pl.BlockSpec((1, D_MODEL, EXPERT_FF), lambda i, be, na: (be[i], 0, 0)),
                  pl.BlockSpec((1, EXPERT_FF, D_MODEL), lambda i, be, na: (be[i], 0, 0))],
        out_specs=rows,
    )
    return pl.pallas_call(
        _experts_kernel,
        grid_spec=grid_spec,
        out_shape=jax.ShapeDtypeStruct((n_rows, D_MODEL), F32),
        compiler_params=_params(1),
        name="experts",
    )(block_expert, n_active, x_sorted, w1, w3, w2)


def _combine_kernel(dest_ref, x_ref, sc_ref, sh_ref, g2_ref, gain_ref, gw_ref, ws1_ref, ws3_ref, ws2_ref,
                    fg_ref, ys_ref, o_ref, rows_ref, sem):
    tm = x_ref.shape[0]

    def start(t, _):
        for k in range(TOP_K):
            _row_copy(ys_ref, dest_ref[k, t], rows_ref.at[k], t, sem).start()
        return 0

    def wait(t, _):
        for k in range(TOP_K):
            _row_copy(ys_ref, dest_ref[k, t], rows_ref.at[k], t, sem).wait()
        return 0

    lax.fori_loop(0, tm, start, 0)
    x = x_ref[...]
    hb = (_rms(x, gain_ref[...]) * (1.0 + sc_ref[0]) + sh_ref[0]).astype(BF16)
    y = _dot((_silu(_dot(hb, ws1_ref[...])) * _dot(hb, ws3_ref[...])).astype(BF16), ws2_ref[...])
    lax.fori_loop(0, tm, wait, 0)
    gw = gw_ref[...]
    for k in range(TOP_K):
        y = y + gw[:, k:k + 1] * rows_ref[k]
    o_ref[...] = _rms(x + g2_ref[0] * y, fg_ref[...])


def _combine(x2, sc, sh, g2, gain, gate_w, dest, y_sorted, ws1, ws3, ws2, final_gain, seq):
    n_tok = x2.shape[0]
    tm = min(MOE_TILE, seq)
    per_seq = seq // tm
    per_b = pl.BlockSpec((1, 1, D_MODEL), lambda i: (i // per_seq, 0, 0))
    tok = pl.BlockSpec((tm, D_MODEL), lambda i: (i, 0))
    return pl.pallas_call(
        _combine_kernel,
        grid=(n_tok // tm,),
        in_specs=[pl.BlockSpec((TOP_K, tm), lambda i: (0, i), memory_space=pltpu.SMEM),
                  tok, per_b, per_b, per_b, _full((1, D_MODEL)),
                  pl.BlockSpec((tm, TOP_K), lambda i: (i, 0)),
                  _full((D_MODEL, SHARED_FF)), _full((D_MODEL, SHARED_FF)), _full((SHARED_FF, D_MODEL)),
                  _full((1, D_MODEL)), pl.BlockSpec(memory_space=pl.ANY)],
        out_specs=tok,
        out_shape=jax.ShapeDtypeStruct((n_tok, D_MODEL), F32),
        scratch_shapes=[pltpu.VMEM((TOP_K, tm, D_MODEL), F32), pltpu.SemaphoreType.DMA(())],
        compiler_params=_params(1),
        name="combine",
    )(dest, x2, sc[:, None, :], sh[:, None, :], g2[:, None, :], gain.reshape(1, D_MODEL), gate_w.T,
      ws1.astype(BF16), ws3.astype(BF16), ws2.astype(BF16), final_gain.reshape(1, D_MODEL), y_sorted)


def _mixer(x, sc1, sh1, g1, p):
    z, xbc, dt_raw, dtT_raw, u = _inproj(x, sc1, sh1, p["norm1_gain"], p["w_in"])
    y_ssd = _ssd(xbc, z, dt_raw, dtT_raw, p["conv_w"], p["conv_b"], p["dt_bias"], p["ssd_a_log"],
                 p["ssd_d"], p["ssd_norm_gain"])
    y_s5 = _s5(u, p["s5_a_re"], p["s5_a_im"], p["s5_log_dt"], p["s5_b_re"], p["s5_b_im"], p["s5_c_re"],
               p["s5_c_im"], p["s5_d"], p["w_glu"], p["b_glu"], p["s5_norm_gain"])
    return _outproj(x, y_ssd, y_s5, g1, p["w_out"])


def _moe(x2, sc2, sh2, g2, p, final_gain, seq):
    n_tok = x2.shape[0]
    eidx, gate_w, pos, counts = _router(x2, sc2, sh2, p["norm2_gain"], p["w_router"], p["router_bias"], seq)
    counts = counts.reshape(N_EXPERTS).astype(I32)
    padded = (counts + EXPERT_BLOCK - 1) // EXPERT_BLOCK * EXPERT_BLOCK
    pad_end = jnp.cumsum(padded)
    pad_start = pad_end - padded
    n_blocks = (n_tok * TOP_K + N_EXPERTS * (EXPERT_BLOCK - 1)) // EXPERT_BLOCK
    block_expert = jnp.minimum(
        jnp.searchsorted(pad_end, jnp.arange(n_blocks, dtype=I32) * EXPERT_BLOCK, side="right"),
        N_EXPERTS - 1).astype(I32)
    n_active = (pad_end[-1:] // EXPERT_BLOCK).astype(I32)
    dest = _dest_rows(eidx, pos, pad_start)
    x_sorted = _dispatch(x2, sc2, sh2, p["norm2_gain"], dest, n_blocks * EXPERT_BLOCK, seq)
    y_sorted = _experts(x_sorted, block_expert, n_active, p["w1"], p["w3"], p["w2"])
    return _combine(x2, sc2, sh2, g2, p["norm2_gain"], gate_w, dest, y_sorted, p["ws1"], p["ws3"], p["ws2"],
                    final_gain, seq)


def kernel(x, c, w_ada, b_ada, norm1_gain, w_in, conv_w, conv_b, dt_bias, ssd_a_log, ssd_d, ssd_norm_gain, s5_a_re, s5_a_im, s5_log_dt, s5_b_re, s5_b_im, s5_c_re, s5_c_im, s5_d, w_glu, b_glu, s5_norm_gain, w_out, norm2_gain, w_router, router_bias, w1, w3, w2, ws1, ws3, ws2, final_gain):
    bsz, seq, d = x.shape
    layers = dict(norm1_gain=norm1_gain, w_in=w_in, conv_w=conv_w, conv_b=conv_b, dt_bias=dt_bias,
                  ssd_a_log=ssd_a_log, ssd_d=ssd_d, ssd_norm_gain=ssd_norm_gain, s5_a_re=s5_a_re,
                  s5_a_im=s5_a_im, s5_log_dt=s5_log_dt, s5_b_re=s5_b_re, s5_b_im=s5_b_im, s5_c_re=s5_c_re,
                  s5_c_im=s5_c_im, s5_d=s5_d, w_glu=w_glu, b_glu=b_glu, s5_norm_gain=s5_norm_gain,
                  w_out=w_out, norm2_gain=norm2_gain, w_router=w_router, router_bias=router_bias,
                  w1=w1, w3=w3, w2=w2, ws1=ws1, ws3=ws3, ws2=ws2)
    depth = w_ada.shape[0]
    assert depth == 1, "the final rmsnorm is fused into the single layer's combine kernel"
    p = {k: v[0] for k, v in layers.items()}
    mod = _modulation(c, w_ada[0], b_ada[0])
    sh1, sc1, g1, sh2, sc2, g2 = [mod[:, i * d:(i + 1) * d] for i in range(N_MOD)]
    x2 = _mixer(x, sc1, sh1, g1, p)
    out = _moe(x2.reshape(bsz * seq, d), sc2, sh2, g2, p, final_gain, seq)
    return out.reshape(bsz, seq, d)
```
